```python
import math
import jax, jax.numpy as jnp
from jax import lax
import numpy as np

D_MODEL = 2048
BATCH = 4
SEQ = 4096
DEPTH = 1

ROPE_THETA = 10000.0
NORM_EPS = 1e-6
Q_BLOCK = 128
NEG_INF = -1e30

MLA_HEADS = 8
MLA_NOPE_DIM = 128
MLA_ROPE_DIM = 64
MLA_QK_DIM = MLA_NOPE_DIM + MLA_ROPE_DIM
MLA_V_DIM = 128
MLA_Q_RANK = 512
MLA_KV_RANK = 256

DIFF_HEADS = 8
DIFF_HEAD_DIM = 64
DIFF_V_DIM = 2 * DIFF_HEAD_DIM

D_FF = 5632
CONV_WIDTH = 3

N_BRANCHES = 2
IN_SPLITS = (
    MLA_Q_RANK,
    MLA_KV_RANK,
    MLA_ROPE_DIM,
    DIFF_HEADS * 2 * DIFF_HEAD_DIM,
    DIFF_HEADS * 2 * DIFF_HEAD_DIM,
    DIFF_HEADS * DIFF_V_DIM,
    N_BRANCHES * D_MODEL,
)
IN_WIDTH = sum(IN_SPLITS)
MIX_WIDTH_MLA = MLA_HEADS * MLA_V_DIM
MIX_WIDTH_DIFF = DIFF_HEADS * DIFF_V_DIM

kernel_name = "hybrid_mla_diffattn_convffn_adaln"


def rms_norm(x, g):
    xf = x.astype(jnp.float32)
    y = xf * lax.rsqrt(jnp.mean(xf * xf, axis=-1, keepdims=True) + NORM_EPS)
    return (y * g.astype(jnp.float32)).astype(x.dtype)


def rope(x, positions):
    d = x.shape[-1]
    inv_freq = ROPE_THETA ** (-jnp.arange(0, d, 2, dtype=jnp.float32) / d)
    ang = positions.astype(jnp.float32)[..., None] * inv_freq
    cos = jnp.cos(ang)[:, :, None, :]
    sin = jnp.sin(ang)[:, :, None, :]
    xf = x.astype(jnp.float32)
    x1, x2 = xf[..., : d // 2], xf[..., d // 2:]
    return jnp.concatenate([x1 * cos - x2 * sin, x2 * cos + x1 * sin], axis=-1).astype(x.dtype)


def _causal_mask(i, seq):
    q_idx = i * Q_BLOCK + jnp.arange(Q_BLOCK)
    return (q_idx[:, None] >= jnp.arange(seq)[None, :])[None, None]


def _unblock(o):
    nb, b, q, h, dv = o.shape
    return jnp.moveaxis(o, 0, 1).reshape(b, nb * q, h, dv)


def causal_softmax_attention(q, k, v):
    seq = q.shape[1]
    scale = q.shape[-1] ** -0.5

    def one_block(i):
        qi = lax.dynamic_slice_in_dim(q, i * Q_BLOCK, Q_BLOCK, axis=1)
        s = jnp.einsum('bqhd,bkhd->bhqk', qi, k).astype(jnp.float32) * scale
        p = jax.nn.softmax(jnp.where(_causal_mask(i, seq), s, NEG_INF), axis=-1)
        return jnp.einsum('bhqk,bkhd->bqhd', p.astype(v.dtype), v)

    return _unblock(lax.map(one_block, jnp.arange(seq // Q_BLOCK)))


def causal_differential_attention(q1, q2, k1, k2, v, lam):
    seq = q1.shape[1]
    scale = q1.shape[-1] ** -0.5

    def one_block(i):
        mask = _causal_mask(i, seq)
        q1i = lax.dynamic_slice_in_dim(q1, i * Q_BLOCK, Q_BLOCK, axis=1)
        q2i = lax.dynamic_slice_in_dim(q2, i * Q_BLOCK, Q_BLOCK, axis=1)
        s1 = jnp.einsum('bqhd,bkhd->bhqk', q1i, k1).astype(jnp.float32) * scale
        s2 = jnp.einsum('bqhd,bkhd->bhqk', q2i, k2).astype(jnp.float32) * scale
        p1 = jax.nn.softmax(jnp.where(mask, s1, NEG_INF), axis=-1)
        p2 = jax.nn.softmax(jnp.where(mask, s2, NEG_INF), axis=-1)
        w = p1 - lam * p2
        return jnp.einsum('bhqk,bkhd->bqhd', w.astype(v.dtype), v)

    return _unblock(lax.map(one_block, jnp.arange(seq // Q_BLOCK)))


def causal_depthwise_conv(u, w, b):
    k = w.shape[0]
    seq = u.shape[1]
    up = jnp.pad(u, ((0, 0), (k - 1, 0), (0, 0)))
    return b + sum(up[:, j:j + seq, :] * w[j] for j in range(k))


def hybrid_layer(layer, x, c, positions, w_ada, b_ada, g_norm1, w_in, b_gate, g_q_lat, w_q_up,
                 g_kv_lat, w_kv_up, g_q_mla, g_k_mla, w_o_mla, g_q_diff, g_k_diff, lam_q1, lam_k1,
                 lam_q2, lam_k2, g_sub_diff, w_o_diff, w_out, g_norm2, w_up, conv_w, conv_b, w_down):
    b_, s_, _ = x.shape
    lambda_init = 0.8 - 0.6 * math.exp(-0.3 * layer)

    mod = jnp.einsum('bd,de->be', jax.nn.silu(c), w_ada) + b_ada
    shift1, scale1, gate1, shift2, scale2, gate2 = jnp.split(mod[:, None, :], 6, axis=-1)

    h = rms_norm(x, g_norm1) * (1.0 + scale1) + shift1
    proj = jnp.einsum('bsd,de->bse', h, w_in)
    offsets = [int(o) for o in np.cumsum(IN_SPLITS)[:-1]]
    q_lat, kv_lat, k_pe, dq, dk, dv, gate_logits = jnp.split(proj, offsets, axis=-1)

    q = jnp.einsum('bsr,re->bse', rms_norm(q_lat, g_q_lat), w_q_up)
    q = q.reshape(b_, s_, MLA_HEADS, MLA_QK_DIM)
    kv = jnp.einsum('bsr,re->bse', rms_norm(kv_lat, g_kv_lat), w_kv_up)
    kv = kv.reshape(b_, s_, MLA_HEADS, MLA_NOPE_DIM + MLA_V_DIM)
    k_nope, v_mla = kv[..., :MLA_NOPE_DIM], kv[..., MLA_NOPE_DIM:]
    q_nope = rms_norm(q[..., :MLA_NOPE_DIM], g_q_mla[:MLA_NOPE_DIM])
    q_pe = rope(rms_norm(q[..., MLA_NOPE_DIM:], g_q_mla[MLA_NOPE_DIM:]), positions)
    k_nope = rms_norm(k_nope, g_k_mla[:MLA_NOPE_DIM])
    k_pe = rope(rms_norm(k_pe[:, :, None, :], g_k_mla[MLA_NOPE_DIM:]), positions)
    k_pe = jnp.broadcast_to(k_pe, (b_, s_, MLA_HEADS, MLA_ROPE_DIM))
    q_mla = jnp.concatenate([q_nope, q_pe], axis=-1)
    k_mla = jnp.concatenate([k_nope, k_pe], axis=-1)
    o_mla = causal_softmax_attention(q_mla, k_mla, v_mla).reshape(b_, s_, MIX_WIDTH_MLA)
    o_mla = jnp.einsum('bse,ed->bsd', o_mla, w_o_mla)

    dq = rope(rms_norm(dq.reshape(b_, s_, 2 * DIFF_HEADS, DIFF_HEAD_DIM), g_q_diff), positions)
    dk = rope(rms_norm(dk.reshape(b_, s_, 2 * DIFF_HEADS, DIFF_HEAD_DIM), g_k_diff), positions)
    dq = dq.reshape(b_, s_, DIFF_HEADS, 2, DIFF_HEAD_DIM)
    dk = dk.reshape(b_, s_, DIFF_HEADS, 2, DIFF_HEAD_DIM)
    dv = dv.reshape(b_, s_, DIFF_HEADS, DIFF_V_DIM)
    f32 = jnp.float32
    lam = (jnp.exp(jnp.sum(lam_q1.astype(f32) * lam_k1.astype(f32)))
           - jnp.exp(jnp.sum(lam_q2.astype(f32) * lam_k2.astype(f32))) + lambda_init)
    o_diff = causal_differential_attention(dq[..., 0, :], dq[..., 1, :], dk[..., 0, :], dk[..., 1, :], dv, lam)
    o_diff = rms_norm(o_diff, g_sub_diff) * (1.0 - lambda_init)
    o_diff = jnp.einsum('bse,ed->bsd', o_diff.reshape(b_, s_, MIX_WIDTH_DIFF), w_o_diff)

    g_a, g_b = jnp.split(jax.nn.sigmoid(gate_logits + b_gate), N_BRANCHES, axis=-1)
    mixed = g_a * o_mla + g_b * o_diff
    x = x + gate1 * jnp.einsum('bsd,de->bse', mixed, w_out)

    h2 = rms_norm(x, g_norm2) * (1.0 + scale2) + shift2
    u = jnp.einsum('bsd,df->bsf', h2, w_up)
    u = causal_depthwise_conv(u, conv_w, conv_b)
    val, gte = jnp.split(u, 2, axis=-1)
    y = jnp.einsum('bsf,fd->bsd', jax.nn.silu(gte) * val, w_down)
    return x + gate2 * y


def setup_inputs(seed: int = 0) -> dict:
    key = jax.random.key(seed)
    ks = iter(jax.random.split(key, 40))

    def w(shape, fan_in, gain=1.0):
        return jax.random.normal(next(ks), (DEPTH,) + shape, jnp.float32) * (gain * fan_in ** -0.5)

    def gain(n):
        return 1.0 + 0.02 * jax.random.normal(next(ks), (DEPTH, n), jnp.float32)

    def bias(shape, s=0.02):
        return s * jax.random.normal(next(ks), (DEPTH,) + shape, jnp.float32)

    x = jax.random.normal(next(ks), (BATCH, SEQ, D_MODEL), jnp.float32)
    c = jax.random.normal(next(ks), (BATCH, D_MODEL), jnp.float32)
    offset = jax.random.randint(next(ks), (BATCH, 1), 0, 1024, dtype=jnp.int32)
    positions = (jnp.arange(SEQ, dtype=jnp.int32)[None, :] + offset).astype(jnp.int32)
    return {
        "x": x,
        "c": c,
        "positions": positions,
        "w_ada": w((D_MODEL, 6 * D_MODEL), D_MODEL, 0.5),
        "b_ada": bias((6 * D_MODEL,)),
        "g_norm1": gain(D_MODEL),
        "w_in": w((D_MODEL, IN_WIDTH), D_MODEL),
        "b_gate": bias((N_BRANCHES * D_MODEL,)),
        "g_q_lat": gain(MLA_Q_RANK),
        "w_q_up": w((MLA_Q_RANK, MLA_HEADS * MLA_QK_DIM), MLA_Q_RANK),
        "g_kv_lat": gain(MLA_KV_RANK),
        "w_kv_up": w((MLA_KV_RANK, MLA_HEADS * (MLA_NOPE_DIM + MLA_V_DIM)), MLA_KV_RANK),
        "g_q_mla": gain(MLA_QK_DIM),
        "g_k_mla": gain(MLA_QK_DIM),
        "w_o_mla": w((MIX_WIDTH_MLA, D_MODEL), MIX_WIDTH_MLA),
        "g_q_diff": gain(DIFF_HEAD_DIM),
        "g_k_diff": gain(DIFF_HEAD_DIM),
        "lam_q1": bias((DIFF_HEAD_DIM,), 0.1),
        "lam_k1": bias((DIFF_HEAD_DIM,), 0.1),
        "lam_q2": bias((DIFF_HEAD_DIM,), 0.1),
        "lam_k2": bias((DIFF_HEAD_DIM,), 0.1),
        "g_sub_diff": gain(DIFF_V_DIM),
        "w_o_diff": w((MIX_WIDTH_DIFF, D_MODEL), MIX_WIDTH_DIFF),
        "w_out": w((D_MODEL, D_MODEL), D_MODEL),
        "g_norm2": gain(D_MODEL),
        "w_up": w((D_MODEL, 2 * D_FF), D_MODEL),
        "conv_w": w((CONV_WIDTH, 2 * D_FF), CONV_WIDTH),
        "conv_b": bias((2 * D_FF,)),
        "w_down": w((D_FF, D_MODEL), D_FF),
    }


def reference(x, c, positions, w_ada, b_ada, g_norm1, w_in, b_gate, g_q_lat, w_q_up, g_kv_lat,
              w_kv_up, g_q_mla, g_k_mla, w_o_mla, g_q_diff, g_k_diff, lam_q1, lam_k1, lam_q2, lam_k2,
              g_sub_diff, w_o_diff, w_out, g_norm2, w_up, conv_w, conv_b, w_down):
    for l in range(DEPTH):
        x = hybrid_layer(l, x, c, positions, w_ada[l], b_ada[l], g_norm1[l], w_in[l], b_gate[l],
                         g_q_lat[l], w_q_up[l], g_kv_lat[l], w_kv_up[l], g_q_mla[l], g_k_mla[l],
                         w_o_mla[l], g_q_diff[l], g_k_diff[l], lam_q1[l], lam_k1[l], lam_q2[l],
                         lam_k2[l], g_sub_diff[l], w_o_diff[l], w_out[l], g_norm2[l], w_up[l],
                         conv_w[l], conv_b[l], w_down[l])
    return x
```

```python
import functools
import math

import jax
import jax.numpy as jnp
from jax import lax
from jax.experimental import pallas as pl
from jax.experimental.pallas import tpu as pltpu

F32 = jnp.float32
BF16 = jnp.bfloat16

D_MODEL = 2048
ROPE_THETA = 10000.0
NORM_EPS = 1e-6
NEG_INF = -1e30

MLA_HEADS = 8
MLA_NOPE_DIM = 128
MLA_ROPE_DIM = 64
MLA_QK_DIM = MLA_NOPE_DIM + MLA_ROPE_DIM
MLA_V_DIM = 128
MLA_Q_RANK = 512
MLA_KV_RANK = 256
MLA_QK_PAD = 256

DIFF_HEADS = 8
DIFF_HEAD_DIM = 64
DIFF_V_DIM = 2 * DIFF_HEAD_DIM
D_FF = 5632
CONV_WIDTH = 3
LAMBDA_INIT = 0.8 - 0.6 * math.exp(-0.3 * 0)

LANES = 128
BF16_SUBLANES = 16
VMEM_LIMIT = 56 * 1024 * 1024

TM_IN = 512
TN_IN = 1024
IN_PAD_WIDTH = 8 * TN_IN
TM_QKV = 512
T_ATT = 512
TM_MERGE = 256
TM_FFN = 512
TF_FFN = 512
HALO = BF16_SUBLANES


def _cparams(sem):
    return pltpu.CompilerParams(dimension_semantics=sem, vmem_limit_bytes=VMEM_LIMIT)


def _rms(x, width):
    ss = jnp.sum(x * x, axis=-1, keepdims=True)
    return x * lax.rsqrt(ss * (1.0 / width) + NORM_EPS)


def _rope128(y, cos, sgn_sin):
    lane = lax.broadcasted_iota(jnp.int32, y.shape, 1)
    partner = jnp.where((lane & 32) == 0, pltpu.roll(y, 96, axis=1), pltpu.roll(y, 32, axis=1))
    return y * cos + partner * sgn_sin


def _rope_table_kernel(pos_ref, invf_ref, sgn_ref, cos_ref, sin_ref):
    ang = pos_ref[...].astype(F32) * invf_ref[...]
    cos_ref[...] = jnp.cos(ang)
    sin_ref[...] = jnp.sin(ang) * sgn_ref[...]


def _rope_table(pos128, invf128, sgn128):
    t = pos128.shape[0]
    tm = 2048
    return pl.pallas_call(
        _rope_table_kernel,
        grid=(t // tm,),
        in_specs=[pl.BlockSpec((tm, LANES), lambda i: (i, 0)),
                  pl.BlockSpec((1, LANES), lambda i: (0, 0)),
                  pl.BlockSpec((1, LANES), lambda i: (0, 0))],
        out_specs=[pl.BlockSpec((tm, LANES), lambda i: (i, 0)),
                   pl.BlockSpec((tm, LANES), lambda i: (i, 0))],
        out_shape=[jax.ShapeDtypeStruct((t, LANES), F32)] * 2,
        compiler_params=_cparams(("parallel",)),
        name="rope_table",
    )(pos128, invf128, sgn128)


def _ada_kernel(c_ref, w_ref, b_ref, o_ref):
    c = c_ref[...]
    a = (c * jax.nn.sigmoid(c)).astype(BF16)
    o_ref[...] = jnp.dot(a, w_ref[...], preferred_element_type=F32) + b_ref[...]


def _ada_mod(c_pad, w_ada, b_ada):
    rows, d = c_pad.shape
    n = w_ada.shape[1]
    tn = 1024
    return pl.pallas_call(
        _ada_kernel,
        grid=(n // tn,),
        in_specs=[pl.BlockSpec((rows, d), lambda j: (0, 0)),
                  pl.BlockSpec((d, tn), lambda j: (0, j)),
                  pl.BlockSpec((1, tn), lambda j: (0, j))],
        out_specs=pl.BlockSpec((rows, tn), lambda j: (0, j)),
        out_shape=jax.ShapeDtypeStruct((rows, n), F32),
        compiler_params=_cparams(("parallel",)),
        name="ada_mod",
    )(c_pad, w_ada, b_ada)


def _in_proj_kernel(x_ref, mod_ref, g1_ref, w_ref, bg_ref, gql_ref, gkvl_ref, gkpe_ref, gqd_ref,
                    gkd_ref, cos_ref, sin_ref, seg_ref,
                    qlat_ref, kvlat_ref, kpe_ref, dq_ref, dk_ref, dvt_ref, gate_ref,
                    h_ref, res_ref):
    n = pl.program_id(1)
    tm = x_ref.shape[0]

    @pl.when(n == 0)
    def _():
        x = x_ref[...]
        hn = _rms(x, D_MODEL) * g1_ref[...]
        h_ref[...] = (hn * (1.0 + mod_ref[1:2, :]) + mod_ref[0:1, :]).astype(BF16)

    res_ref[...] = jnp.dot(h_ref[...], w_ref[...], preferred_element_type=F32)

    def norm_rope_groups(out_ref, gain_ref, scale):
        cos = cos_ref[...]
        sin = sin_ref[...]
        for g in range(TN_IN // LANES):
            xg = res_ref[:, g * LANES:(g + 1) * LANES]
            ss = jnp.dot((xg * xg).astype(BF16), seg_ref[...], preferred_element_type=F32)
            y = xg * lax.rsqrt(ss * (1.0 / DIFF_HEAD_DIM) + NORM_EPS) * gain_ref[...]
            y = _rope128(y, cos, sin)
            if scale != 1.0:
                y = y * scale
            out_ref[:, g * LANES:(g + 1) * LANES] = y.astype(BF16)

    @pl.when(n == 0)
    def _():
        ql = res_ref[:, 0:MLA_Q_RANK]
        qlat_ref[...] = (_rms(ql, MLA_Q_RANK) * gql_ref[...]).astype(BF16)
        o = MLA_Q_RANK
        kvl = res_ref[:, o:o + MLA_KV_RANK]
        kvlat_ref[...] = (_rms(kvl, MLA_KV_RANK) * gkvl_ref[...]).astype(BF16)
        o += MLA_KV_RANK
        kp = res_ref[:, o:o + LANES]
        y = _rms(kp, MLA_ROPE_DIM) * gkpe_ref[...]
        kpe_ref[...] = _rope128(y, cos_ref[...], sin_ref[...]).astype(BF16)

    @pl.when(n == 1)
    def _():
        norm_rope_groups(dq_ref, gqd_ref, DIFF_HEAD_DIM ** -0.5)

    @pl.when(n == 2)
    def _():
        norm_rope_groups(dk_ref, gkd_ref, 1.0)

    @pl.when(n == 3)
    def _():
        for hd in range(DIFF_HEADS):
            for t in range(tm // T_ATT):
                blk = res_ref[t * T_ATT:(t + 1) * T_ATT, hd * LANES:(hd + 1) * LANES]
                dvt_ref[hd, t] = blk.T.astype(BF16)

    for k in range(4):
        @pl.when(n == 4 + k)
        def _(k=k):
            z = res_ref[...] + bg_ref[:, k * TN_IN:(k + 1) * TN_IN]
            gate_ref[:, k * TN_IN:(k + 1) * TN_IN] = jax.nn.sigmoid(z).astype(BF16)


def _in_proj(x2, mod3, g1, w_in_p, b_gate, gql, gkvl, gkpe, gqd, gkd, cos, sin, seg, batch, seq):
    t = x2.shape[0]
    tm = TM_IN
    tiles_per_seq = seq // tm
    nk = seq // T_ATT
    row = lambda i, n: (i, 0)
    const = lambda i, n: (0, 0)
    out_shape = [
        jax.ShapeDtypeStruct((t, MLA_Q_RANK), BF16),
        jax.ShapeDtypeStruct((t, MLA_KV_RANK), BF16),
        jax.ShapeDtypeStruct((t, LANES), BF16),
        jax.ShapeDtypeStruct((t, DIFF_HEADS * LANES), BF16),
        jax.ShapeDtypeStruct((t, DIFF_HEADS * LANES), BF16),
        jax.ShapeDtypeStruct((batch, DIFF_HEADS, nk, DIFF_V_DIM, T_ATT), BF16),
        jax.ShapeDtypeStruct((t, 2 * D_MODEL), BF16),
    ]
    out_specs = [
        pl.BlockSpec((tm, MLA_Q_RANK), row),
        pl.BlockSpec((tm, MLA_KV_RANK), row),
        pl.BlockSpec((tm, LANES), row),
        pl.BlockSpec((tm, DIFF_HEADS * LANES), row),
        pl.BlockSpec((tm, DIFF_HEADS * LANES), row),
        pl.BlockSpec((None, DIFF_HEADS, tm // T_ATT, DIFF_V_DIM, T_ATT),
                     lambda i, n: (i // tiles_per_seq, 0, i % tiles_per_seq, 0, 0)),
        pl.BlockSpec((tm, 2 * D_MODEL), row),
    ]
    in_specs = [
        pl.BlockSpec((tm, D_MODEL), row),
        pl.BlockSpec((None, 6, D_MODEL), lambda i, n: (i // tiles_per_seq, 0, 0)),
        pl.BlockSpec((1, D_MODEL), const),
        pl.BlockSpec((D_MODEL, TN_IN), lambda i, n: (0, n)),
        pl.BlockSpec((1, 2 * D_MODEL), const),
        pl.BlockSpec((1, MLA_Q_RANK), const),
        pl.BlockSpec((1, MLA_KV_RANK), const),
        pl.BlockSpec((1, LANES), const),
        pl.BlockSpec((1, LANES), const),
        pl.BlockSpec((1, LANES), const),
        pl.BlockSpec((tm, LANES), row),
        pl.BlockSpec((tm, LANES), row),
        pl.BlockSpec((LANES, LANES), const),
    ]
    return pl.pallas_call(
        _in_proj_kernel,
        grid=(t // tm, IN_PAD_WIDTH // TN_IN),
        in_specs=in_specs,
        out_specs=out_specs,
        out_shape=out_shape,
        scratch_shapes=[pltpu.VMEM((tm, D_MODEL), BF16), pltpu.VMEM((tm, TN_IN), F32)],
        compiler_params=_cparams(("parallel", "arbitrary")),
        name="in_proj",
    )(x2, mod3, g1, w_in_p, b_gate, gql, gkvl, gkpe, gqd, gkd, cos, sin, seg)


def _mla_qkv_kernel(qlat_ref, kvlat_ref, kpe_ref, wq_ref, wk_ref, wvt_ref, gqn_ref, gqp_ref, gkn_ref,
                    cos_ref, sin_ref, q_ref, k_ref, vt_ref):
    tm = qlat_ref.shape[0]
    scale = MLA_QK_DIM ** -0.5
    q = jnp.dot(qlat_ref[...], wq_ref[...], preferred_element_type=F32)
    kn = jnp.dot(kvlat_ref[...], wk_ref[...], preferred_element_type=F32)
    cos = cos_ref[...]
    sin = sin_ref[...]
    kpe = kpe_ref[...]
    kvl = kvlat_ref[...]
    for hd in range(MLA_HEADS):
        base = hd * MLA_QK_PAD
        qn = q[:, base:base + LANES]
        qp = q[:, base + LANES:base + 2 * LANES]
        qn = _rms(qn, MLA_NOPE_DIM) * gqn_ref[...]
        qp = _rope128(_rms(qp, MLA_ROPE_DIM) * gqp_ref[...], cos, sin)
        q_ref[hd, :, 0:LANES] = (qn * scale).astype(BF16)
        q_ref[hd, :, LANES:2 * LANES] = (qp * scale).astype(BF16)
        kh = kn[:, hd * LANES:(hd + 1) * LANES]
        k_ref[hd, :, 0:LANES] = (_rms(kh, MLA_NOPE_DIM) * gkn_ref[...]).astype(BF16)
        k_ref[hd, :, LANES:2 * LANES] = kpe
        vt = lax.dot_general(wvt_ref[hd], kvl, (((1,), (1,)), ((), ())), preferred_element_type=F32)
        for t in range(tm // T_ATT):
            vt_ref[hd, t] = vt[:, t * T_ATT:(t + 1) * T_ATT].astype(BF16)


def _mla_qkv(qlat, kvlat, kpe, wq, wk, wvt, gqn, gqp, gkn, cos, sin, batch, seq):
    t = qlat.shape[0]
    tm = TM_QKV
    tiles_per_seq = seq // tm
    nk = seq // T_ATT
    row = lambda i: (i, 0)
    const2 = lambda i: (0, 0)
    head_rows = lambda i: (i // tiles_per_seq, 0, i % tiles_per_seq, 0)
    return pl.pallas_call(
        _mla_qkv_kernel,
        grid=(t // tm,),
        in_specs=[
            pl.BlockSpec((tm, MLA_Q_RANK), row),
            pl.BlockSpec((tm, MLA_KV_RANK), row),
            pl.BlockSpec((tm, LANES), row),
            pl.BlockSpec(wq.shape, const2),
            pl.BlockSpec(wk.shape, const2),
            pl.BlockSpec(wvt.shape, lambda i: (0, 0, 0)),
            pl.BlockSpec((1, LANES), const2),
            pl.BlockSpec((1, LANES), const2),
            pl.BlockSpec((1, LANES), const2),
            pl.BlockSpec((tm, LANES), row),
            pl.BlockSpec((tm, LANES), row),
        ],
        out_specs=[
            pl.BlockSpec((None, MLA_HEADS, tm, MLA_QK_PAD), head_rows),
            pl.BlockSpec((None, MLA_HEADS, tm, MLA_QK_PAD), head_rows),
            pl.BlockSpec((None, MLA_HEADS, tm // T_ATT, MLA_V_DIM, T_ATT),
                         lambda i: (i // tiles_per_seq, 0, i % tiles_per_seq, 0, 0)),
        ],
        out_shape=[
            jax.ShapeDtypeStruct((batch, MLA_HEADS, seq, MLA_QK_PAD), BF16),
            jax.ShapeDtypeStruct((batch, MLA_HEADS, seq, MLA_QK_PAD), BF16),
            jax.ShapeDtypeStruct((batch, MLA_HEADS, nk, MLA_V_DIM, T_ATT), BF16),
        ],
        compiler_params=_cparams(("parallel",)),
        name="mla_qkv",
    )(qlat, kvlat, kpe, wq, wk, wvt, gqn, gqp, gkn, cos, sin)


def _softmax_tile(st, vt, m_ref, l_ref, acc_ref):
    m_prev = m_ref[...]
    m_new = jnp.maximum(m_prev, jnp.max(st, axis=0, keepdims=True))
    alpha = jnp.exp(m_prev - m_new)
    p = jnp.exp(st - m_new)
    l_ref[...] = alpha * l_ref[...] + jnp.sum(p, axis=0, keepdims=True)
    acc_ref[...] = alpha * acc_ref[...] + jnp.dot(vt, p.astype(BF16), preferred_element_type=F32)
    m_ref[...] = m_new


def _causal_keep(shape):
    key = lax.broadcasted_iota(jnp.int32, shape, 0)
    qry = lax.broadcasted_iota(jnp.int32, shape, 1)
    return key <= qry


_NT = (((1,), (1,)), ((), ()))


def _mla_attn_kernel(q_ref, k_ref, vt_ref, o_ref, m_ref, l_ref, acc_ref):
    i = pl.program_id(2)
    q = q_ref[...]
    m_ref[...] = jnp.full(m_ref.shape, NEG_INF, F32)
    l_ref[...] = jnp.zeros(l_ref.shape, F32)
    acc_ref[...] = jnp.zeros(acc_ref.shape, F32)

    def tile(j, masked):
        k = k_ref[pl.ds(pl.multiple_of(j * T_ATT, T_ATT), T_ATT), :]
        st = lax.dot_general(k, q, _NT, preferred_element_type=F32)
        if masked:
            st = jnp.where(_causal_keep(st.shape), st, NEG_INF)
        _softmax_tile(st, vt_ref[j], m_ref, l_ref, acc_ref)

    def body(j, carry):
        tile(j, False)
        return carry

    lax.fori_loop(0, i, body, 0)
    tile(i, True)
    o = acc_ref[...] * (1.0 / l_ref[...])
    o_ref[...] = o.T.astype(BF16)


def _mla_attention(q, k, vt, batch, seq):
    nq = seq // T_ATT
    return pl.pallas_call(
        _mla_attn_kernel,
        grid=(batch, MLA_HEADS, nq),
        in_specs=[
            pl.BlockSpec((None, None, T_ATT, MLA_QK_PAD), lambda b, h, i: (b, h, i, 0)),
            pl.BlockSpec((None, None, seq, MLA_QK_PAD), lambda b, h, i: (b, h, 0, 0)),
            pl.BlockSpec((None, None, nq, MLA_V_DIM, T_ATT), lambda b, h, i: (b, h, 0, 0, 0)),
        ],
        out_specs=pl.BlockSpec((None, T_ATT, MLA_V_DIM), lambda b, h, i: (b, i, h)),
        out_shape=jax.ShapeDtypeStruct((batch, seq, MLA_HEADS * MLA_V_DIM), BF16),
        scratch_shapes=[pltpu.VMEM((1, T_ATT), F32), pltpu.VMEM((1, T_ATT), F32),
                        pltpu.VMEM((MLA_V_DIM, T_ATT), F32)],
        compiler_params=_cparams(("parallel", "parallel", "arbitrary")),
        name="mla_attn",
    )(q, k, vt)


def _diff_attn_kernel(q_ref, k_ref, vt_ref, lam_ref, gsub_ref, o_ref,
                      m1_ref, l1_ref, acc1_ref, m2_ref, l2_ref, acc2_ref):
    i = pl.program_id(2)
    q = q_ref[...]
    lane = lax.broadcasted_iota(jnp.int32, q.shape, 1)
    zero = jnp.zeros_like(q)
    q1 = jnp.where(lane < DIFF_HEAD_DIM, q, zero)
    q2 = jnp.where(lane >= DIFF_HEAD_DIM, q, zero)
    for m_ref, l_ref, acc_ref in ((m1_ref, l1_ref, acc1_ref), (m2_ref, l2_ref, acc2_ref)):
        m_ref[...] = jnp.full(m_ref.shape, NEG_INF, F32)
        l_ref[...] = jnp.zeros(l_ref.shape, F32)
        acc_ref[...] = jnp.zeros(acc_ref.shape, F32)

    def tile(j, masked):
        k = k_ref[pl.ds(pl.multiple_of(j * T_ATT, T_ATT), T_ATT), :]
        vt = vt_ref[j]
        for qm, m_ref, l_ref, acc_ref in ((q1, m1_ref, l1_ref, acc1_ref), (q2, m2_ref, l2_ref, acc2_ref)):
            st = lax.dot_general(k, qm, _NT, preferred_element_type=F32)
            if masked:
                st = jnp.where(_causal_keep(st.shape), st, NEG_INF)
            _softmax_tile(st, vt, m_ref, l_ref, acc_ref)

    def body(j, carry):
        tile(j, False)
        return carry

    lax.fori_loop(0, i, body, 0)
    tile(i, True)

    lv = lam_ref[...]
    lam = (jnp.exp(jnp.sum(lv[0:1, :] * lv[1:2, :], axis=-1, keepdims=True))
           - jnp.exp(jnp.sum(lv[2:3, :] * lv[3:4, :], axis=-1, keepdims=True)) + LAMBDA_INIT)
    o = acc1_ref[...] * (1.0 / l1_ref[...]) - lam * (acc2_ref[...] * (1.0 / l2_ref[...]))
    ot = o.T
    ot = _rms(ot, DIFF_V_DIM) * gsub_ref[...] * (1.0 - LAMBDA_INIT)
    o_ref[...] = ot.astype(BF16)


def _diff_attention(dq, dk, dvt, lam4, gsub, batch, seq):
    nq = seq // T_ATT
    return pl.pallas_call(
        _diff_attn_kernel,
        grid=(batch, DIFF_HEADS, nq),
        in_specs=[
            pl.BlockSpec((None, T_ATT, LANES), lambda b, h, i: (b, i, h)),
            pl.BlockSpec((None, seq, LANES), lambda b, h, i: (b, 0, h)),
            pl.BlockSpec((None, None, nq, DIFF_V_DIM, T_ATT), lambda b, h, i: (b, h, 0, 0, 0)),
            pl.BlockSpec(lam4.shape, lambda b, h, i: (0, 0)),
            pl.BlockSpec((1, LANES), lambda b, h, i: (0, 0)),
        ],
        out_specs=pl.BlockSpec((None, T_ATT, DIFF_V_DIM), lambda b, h, i: (b, i, h)),
        out_shape=jax.ShapeDtypeStruct((batch, seq, DIFF_HEADS * DIFF_V_DIM), BF16),
        scratch_shapes=[pltpu.VMEM((1, T_ATT), F32), pltpu.VMEM((1, T_ATT), F32),
                        pltpu.VMEM((DIFF_V_DIM, T_ATT), F32),
                        pltpu.VMEM((1, T_ATT), F32), pltpu.VMEM((1, T_ATT), F32),
                        pltpu.VMEM((DIFF_V_DIM, T_ATT), F32)],
        compiler_params=_cparams(("parallel", "parallel", "arbitrary")),
        name="diff_attn",
    )(dq, dk, dvt, lam4, gsub)


def _merge_kernel(x_ref, mod_ref, oa_ref, ob_ref, gate_ref, woa_ref, wob_ref, wout_ref, o_ref):
    a = jnp.dot(oa_ref[...], woa_ref[...], preferred_element_type=F32)
    b = jnp.dot(ob_ref[...], wob_ref[...], preferred_element_type=F32)
    ga = gate_ref[:, 0:D_MODEL].astype(F32)
    gb = gate_ref[:, D_MODEL:2 * D_MODEL].astype(F32)
    mixed = (ga * a + gb * b).astype(BF16)
    y = jnp.dot(mixed, wout_ref[...], preferred_element_type=F32)
    o_ref[...] = x_ref[...] + mod_ref[2:3, :] * y


def _merge(x2, mod3, oa, ob, gates, woa, wob, wout, seq):
    t = x2.shape[0]
    tm = TM_MERGE
    tiles_per_seq = seq // tm
    row = lambda i: (i, 0)
    const = lambda i: (0, 0)
    resident = dict(pipeline_mode=pl.Buffered(1))
    return pl.pallas_call(
        _merge_kernel,
        grid=(t // tm,),
        in_specs=[
            pl.BlockSpec((tm, D_MODEL), row),
            pl.BlockSpec((None, 6, D_MODEL), lambda i: (i // tiles_per_seq, 0, 0)),
            pl.BlockSpec((tm, oa.shape[1]), row),
            pl.BlockSpec((tm, ob.shape[1]), row),
            pl.BlockSpec((tm, 2 * D_MODEL), row),
            pl.BlockSpec(woa.shape, const, **resident),
            pl.BlockSpec(wob.shape, const, **resident),
            pl.BlockSpec(wout.shape, const, **resident),
        ],
        out_specs=pl.BlockSpec((tm, D_MODEL), row),
        out_shape=jax.ShapeDtypeStruct((t, D_MODEL), F32),
        compiler_params=_cparams(("parallel",)),
        name="merge_out",
    )(x2, mod3, oa, ob, gates, woa, wob, wout)


def _ffn_kernel(x_ref, halo_ref, mod_ref, g2_ref, wv_ref, wg_ref, cwv_ref, cwg_ref, cbv_ref, cbg_ref,
                wd_ref, o_ref, h_ref, *, tiles_per_seq):
    i = pl.program_id(0)
    j = pl.program_id(1)
    tm = x_ref.shape[0]

    @pl.when(j == 0)
    def _():
        scale = 1.0 + mod_ref[4:5, :]
        shift = mod_ref[3:4, :]
        h_ref[HALO:, :] = (_rms(x_ref[...], D_MODEL) * g2_ref[...] * scale + shift).astype(BF16)
        hh = (_rms(halo_ref[...], D_MODEL) * g2_ref[...] * scale + shift).astype(BF16)
        first = (i % tiles_per_seq) == 0
        h_ref[0:HALO, :] = jnp.where(first, jnp.zeros_like(hh), hh)
        o_ref[...] = jnp.zeros(o_ref.shape, F32)

    h = h_ref[...]

    def conv(w_ref, cw_ref, cb_ref):
        u = jnp.dot(h, w_ref[...], preferred_element_type=F32)
        u1 = pltpu.roll(u, 1, axis=0)
        u2 = pltpu.roll(u, 2, axis=0)
        return (cb_ref[...] + cw_ref[2:3, :] * u[HALO:, :] + cw_ref[1:2, :] * u1[HALO:, :]
                + cw_ref[0:1, :] * u2[HALO:, :])

    val = conv(wv_ref, cwv_ref, cbv_ref)
    gte = conv(wg_ref, cwg_ref, cbg_ref)
    act = (gte * jax.nn.sigmoid(gte) * val).astype(BF16)
    o_ref[...] += jnp.dot(act, wd_ref[...], preferred_element_type=F32)

    @pl.when(j == pl.num_programs(1) - 1)
    def _():
        o_ref[...] = x_ref[...] + mod_ref[5:6, :] * o_ref[...]


def _ffn(x1, mod3, g2, wv, wg, cwv, cwg, cbv, cbg, wd, seq):
    t = x1.shape[0]
    tm, tf = TM_FFN, TF_FFN
    tiles_per_seq = seq // tm
    halo_blocks = tm // HALO
    row = lambda i, j: (i, 0)
    return pl.pallas_call(
        functools.partial(_ffn_kernel, tiles_per_seq=tiles_per_seq),
        grid=(t // tm, D_FF // tf),
        in_specs=[
            pl.BlockSpec((tm, D_MODEL), row),
            pl.BlockSpec((HALO, D_MODEL), lambda i, j: (jnp.maximum(i * halo_blocks - 1, 0), 0)),
            pl.BlockSpec((None, 6, D_MODEL), lambda i, j: (i // tiles_per_seq, 0, 0)),
            pl.BlockSpec((1, D_MODEL), lambda i, j: (0, 0)),
            pl.BlockSpec((D_MODEL, tf), lambda i, j: (0, j)),
            pl.BlockSpec((D_MODEL, tf), lambda i, j: (0, j)),
            pl.BlockSpec((CONV_WIDTH, tf), lambda i, j: (0, j)),
            pl.BlockSpec((CONV_WIDTH, tf), lambda i, j: (0, j)),
            pl.BlockSpec((1, tf), lambda i, j: (0, j)),
            pl.BlockSpec((1, tf), lambda i, j: (0, j)),
            pl.BlockSpec((tf, D_MODEL), lambda i, j: (j, 0)),
        ],
        out_specs=pl.BlockSpec((tm, D_MODEL), row),
        out_shape=jax.ShapeDtypeStruct((t, D_MODEL), F32),
        scratch_shapes=[pltpu.VMEM((HALO + tm, D_MODEL), BF16)],
        compiler_params=_cparams(("parallel", "arbitrary")),
        name="conv_ffn",
    )(x1, x1, mod3, g2, wv, wg, cwv, cwg, cbv, cbg, wd)


def _layer(x, c, positions, w_ada, b_ada, g_norm1, w_in, b_gate, g_q_lat, w_q_up, g_kv_lat, w_kv_up,
           g_q_mla, g_k_mla, w_o_mla, g_q_diff, g_k_diff, lam_q1, lam_k1, lam_q2, lam_k2, g_sub_diff,
           w_o_diff, w_out, g_norm2, w_up, conv_w, conv_b, w_down):
    batch, seq, d = x.shape
    t = batch * seq
    assert d == D_MODEL and seq % TM_IN == 0 and seq % T_ATT == 0 and TM_IN % T_ATT == 0
    assert seq % TM_FFN == 0 and seq % TM_MERGE == 0 and seq % TM_QKV == 0 and TM_QKV % T_ATT == 0
    assert t % 2048 == 0 and D_FF % TF_FFN == 0
    x2 = x.reshape(t, d)
    row = lambda v: v.reshape(1, -1).astype(F32)

    half = DIFF_HEAD_DIM // 2
    inv_freq = ROPE_THETA ** (-jnp.arange(0, DIFF_HEAD_DIM, 2, dtype=F32) / DIFF_HEAD_DIM)
    invf128 = jnp.tile(inv_freq, LANES // half).reshape(1, LANES)
    sgn128 = jnp.tile(jnp.concatenate([-jnp.ones((half,), F32), jnp.ones((half,), F32)]),
                      LANES // DIFF_HEAD_DIM).reshape(1, LANES)
    pos128 = jnp.broadcast_to(positions.reshape(t, 1), (t, LANES))
    cos, sin = _rope_table(pos128, invf128, sgn128)

    c_pad = jnp.zeros((8, d), F32).at[:batch].set(c)
    mod = _ada_mod(c_pad, w_ada.astype(BF16), row(b_ada))
    mod3 = mod[:batch].reshape(batch, 6, d)

    small = MLA_Q_RANK + MLA_KV_RANK + MLA_ROPE_DIM
    w_in_p = jnp.concatenate([w_in[:, :small], jnp.zeros((d, TN_IN - small), w_in.dtype),
                              w_in[:, small:]], axis=1).astype(BF16)
    pad64 = jnp.zeros((LANES - MLA_ROPE_DIM,), F32)
    gkpe = jnp.concatenate([g_k_mla[MLA_NOPE_DIM:], pad64]).reshape(1, LANES)
    seg = (jnp.arange(LANES)[:, None] // DIFF_HEAD_DIM == jnp.arange(LANES)[None, :] // DIFF_HEAD_DIM)
    qlat, kvlat, kpe, dq, dk, dvt, gates = _in_proj(
        x2, mod3, row(g_norm1), w_in_p, row(b_gate), row(g_q_lat), row(g_kv_lat), gkpe,
        row(jnp.tile(g_q_diff, 2)), row(jnp.tile(g_k_diff, 2)), cos, sin, seg.astype(BF16), batch, seq)

    wq = w_q_up.reshape(MLA_Q_RANK, MLA_HEADS, MLA_QK_DIM)
    wq = jnp.pad(wq, ((0, 0), (0, 0), (0, MLA_QK_PAD - MLA_QK_DIM)))
    wq = wq.reshape(MLA_Q_RANK, MLA_HEADS * MLA_QK_PAD).astype(BF16)
    wkv = w_kv_up.reshape(MLA_KV_RANK, MLA_HEADS, MLA_NOPE_DIM + MLA_V_DIM)
    wk = wkv[:, :, :MLA_NOPE_DIM].reshape(MLA_KV_RANK, MLA_HEADS * MLA_NOPE_DIM).astype(BF16)
    wvt = jnp.transpose(wkv[:, :, MLA_NOPE_DIM:], (1, 2, 0)).astype(BF16)
    gqp = jnp.concatenate([g_q_mla[MLA_NOPE_DIM:], pad64]).reshape(1, LANES)
    q_mla, k_mla, vt_mla = _mla_qkv(qlat, kvlat, kpe, wq, wk, wvt, row(g_q_mla[:MLA_NOPE_DIM]), gqp,
                                    row(g_k_mla[:MLA_NOPE_DIM]), cos, sin, batch, seq)

    o_mla = _mla_attention(q_mla, k_mla, vt_mla, batch, seq)
    lam4 = jnp.stack([lam_q1, lam_k1, lam_q2, lam_k2]).astype(F32)
    o_diff = _diff_attention(dq.reshape(batch, seq, -1), dk.reshape(batch, seq, -1), dvt, lam4,
                             row(g_sub_diff), batch, seq)

    x1 = _merge(x2, mod3, o_mla.reshape(t, -1), o_diff.reshape(t, -1), gates,
                w_o_mla.astype(BF16), w_o_diff.astype(BF16), w_out.astype(BF16), seq)

    out = _ffn(x1, mod3, row(g_norm2), w_up[:, :D_FF].astype(BF16), w_up[:, D_FF:].astype(BF16),
               conv_w[:, :D_FF], conv_w[:, D_FF:], row(conv_b[:D_FF]), row(conv_b[D_FF:]),
               w_down.astype(BF16), seq)
    return out.reshape(batch, seq, d)


def kernel(x, c, positions, w_ada, b_ada, g_norm1, w_in, b_gate, g_q_lat, w_q_up, g_kv_lat, w_kv_up, g_q_mla, g_k_mla, w_o_mla, g_q_diff, g_k_diff, lam_q1, lam_k1, lam_q2, lam_k2, g_sub_diff, w_o_diff, w_out, g_norm2, w_up, conv_w, conv_b, w_down):
    depth = w_ada.shape[0]
    assert depth == 1, "lambda_init is derived for a single layer"
    return _layer(x, c, positions, w_ada[0], b_ada[0], g_norm1[0], w_in[0], b_gate[0], g_q_lat[0],
                  w_q_up[0], g_kv_lat[0], w_kv_up[0], g_q_mla[0], g_k_mla[0], w_o_mla[0], g_q_diff[0],
                  g_k_diff[0], lam_q1[0], lam_k1[0], lam_q2[0], lam_k2[0], g_sub_diff[0], w_o_diff[0],
                  w_out[0], g_norm2[0], w_up[0], conv_w[0], conv_b[0], w_down[0])
```

```python
import functools
import math

import jax
import jax.numpy as jnp
from jax import lax
from jax.experimental import pallas as pl
from jax.experimental.pallas import tpu as pltpu

F32 = jnp.float32
BF16 = jnp.bfloat16

D_MODEL = 2048
ROPE_THETA = 10000.0
NORM_EPS = 1e-6
NEG_INF = -1e30

MLA_HEADS = 8
MLA_NOPE_DIM = 128
MLA_ROPE_DIM = 64
MLA_QK_DIM = MLA_NOPE_DIM + MLA_ROPE_DIM
MLA_V_DIM = 128
MLA_Q_RANK = 512
MLA_KV_RANK = 256
MLA_QK_PAD = 256

DIFF_HEADS = 8
DIFF_HEAD_DIM = 64
DIFF_V_DIM = 2 * DIFF_HEAD_DIM
D_FF = 5632
CONV_WIDTH = 3
LAMBDA_INIT = 0.8 - 0.6 * math.exp(-0.3 * 0)
LOG2E = math.log2(math.e)

LANES = 128
BF16_SUBLANES = 16
VMEM_LIMIT = 56 * 1024 * 1024

TM_IN = 512
TN_IN = 1024
IN_PAD_WIDTH = 8 * TN_IN
TM_QKV = 512
T_ATT = 512
TM_MERGE = 256
TM_FFN = 512
TF_FFN = 512
HALO = BF16_SUBLANES


def _cparams(sem):
    return pltpu.CompilerParams(dimension_semantics=sem, vmem_limit_bytes=VMEM_LIMIT)


def _rms(x, width):
    ss = jnp.sum(x * x, axis=-1, keepdims=True)
    return x * lax.rsqrt(ss * (1.0 / width) + NORM_EPS)


def _rope128(y, cos, sgn_sin):
    lane = lax.broadcasted_iota(jnp.int32, y.shape, 1)
    partner = jnp.where((lane & 32) == 0, pltpu.roll(y, 96, axis=1), pltpu.roll(y, 32, axis=1))
    return y * cos + partner * sgn_sin


def _rope_table_kernel(pos_ref, invf_ref, sgn_ref, cos_ref, sin_ref):
    ang = pos_ref[...].astype(F32) * invf_ref[...]
    cos_ref[...] = jnp.cos(ang)
    sin_ref[...] = jnp.sin(ang) * sgn_ref[...]


def _rope_table(pos128, invf128, sgn128):
    t = pos128.shape[0]
    tm = 2048
    return pl.pallas_call(
        _rope_table_kernel,
        grid=(t // tm,),
        in_specs=[pl.BlockSpec((tm, LANES), lambda i: (i, 0)),
                  pl.BlockSpec((1, LANES), lambda i: (0, 0)),
                  pl.BlockSpec((1, LANES), lambda i: (0, 0))],
        out_specs=[pl.BlockSpec((tm, LANES), lambda i: (i, 0)),
                   pl.BlockSpec((tm, LANES), lambda i: (i, 0))],
        out_shape=[jax.ShapeDtypeStruct((t, LANES), F32)] * 2,
        compiler_params=_cparams(("parallel",)),
        name="rope_table",
    )(pos128, invf128, sgn128)


def _ada_kernel(c_ref, w_ref, b_ref, o_ref):
    c = c_ref[...]
    a = (c * jax.nn.sigmoid(c)).astype(BF16)
    o_ref[...] = jnp.dot(a, w_ref[...], preferred_element_type=F32) + b_ref[...]


def _ada_mod(c_pad, w_ada, b_ada):
    rows, d = c_pad.shape
    n = w_ada.shape[1]
    tn = 1024
    return pl.pallas_call(
        _ada_kernel,
        grid=(n // tn,),
        in_specs=[pl.BlockSpec((rows, d), lambda j: (0, 0)),
                  pl.BlockSpec((d, tn), lambda j: (0, j)),
                  pl.BlockSpec((1, tn), lambda j: (0, j))],
        out_specs=pl.BlockSpec((rows, tn), lambda j: (0, j)),
        out_shape=jax.ShapeDtypeStruct((rows, n), F32),
        compiler_params=_cparams(("parallel",)),
        name="ada_mod",
    )(c_pad, w_ada, b_ada)


def _in_proj_kernel(x_ref, mod_ref, g1_ref, w_ref, bg_ref, gql_ref, gkvl_ref, gkpe_ref, gqd_ref,
                    gkd_ref, cos_ref, sin_ref, seg_ref,
                    qlat_ref, kvlat_ref, kpe_ref, dq_ref, dk_ref, dvt_ref, gate_ref,
                    h_ref, res_ref):
    n = pl.program_id(1)
    tm = x_ref.shape[0]

    @pl.when(n == 0)
    def _():
        x = x_ref[...]
        hn = _rms(x, D_MODEL) * g1_ref[...]
        h_ref[...] = (hn * (1.0 + mod_ref[1:2, :]) + mod_ref[0:1, :]).astype(BF16)

    res_ref[...] = jnp.dot(h_ref[...], w_ref[...], preferred_element_type=F32)

    def norm_rope_groups(out_ref, gain_ref, scale):
        cos = cos_ref[...]
        sin = sin_ref[...]
        for g in range(TN_IN // LANES):
            xg = res_ref[:, g * LANES:(g + 1) * LANES]
            ss = jnp.dot((xg * xg).astype(BF16), seg_ref[...], preferred_element_type=F32)
            y = xg * lax.rsqrt(ss * (1.0 / DIFF_HEAD_DIM) + NORM_EPS) * gain_ref[...]
            y = _rope128(y, cos, sin)
            if scale != 1.0:
                y = y * scale
            out_ref[:, g * LANES:(g + 1) * LANES] = y.astype(BF16)

    @pl.when(n == 0)
    def _():
        ql = res_ref[:, 0:MLA_Q_RANK]
        qlat_ref[...] = (_rms(ql, MLA_Q_RANK) * gql_ref[...]).astype(BF16)
        o = MLA_Q_RANK
        kvl = res_ref[:, o:o + MLA_KV_RANK]
        kvlat_ref[...] = (_rms(kvl, MLA_KV_RANK) * gkvl_ref[...]).astype(BF16)
        o += MLA_KV_RANK
        kp = res_ref[:, o:o + LANES]
        y = _rms(kp, MLA_ROPE_DIM) * gkpe_ref[...]
        kpe_ref[...] = _rope128(y, cos_ref[...], sin_ref[...]).astype(BF16)

    @pl.when(n == 1)
    def _():
        norm_rope_groups(dq_ref, gqd_ref, DIFF_HEAD_DIM ** -0.5 * LOG2E)

    @pl.when(n == 2)
    def _():
        norm_rope_groups(dk_ref, gkd_ref, 1.0)

    @pl.when(n == 3)
    def _():
        for hd in range(DIFF_HEADS):
            for t in range(tm // T_ATT):
                blk = res_ref[t * T_ATT:(t + 1) * T_ATT, hd * LANES:(hd + 1) * LANES]
                dvt_ref[hd, t] = blk.T.astype(BF16)

    for k in range(4):
        @pl.when(n == 4 + k)
        def _(k=k):
            z = res_ref[...] + bg_ref[:, k * TN_IN:(k + 1) * TN_IN]
            gate_ref[:, k * TN_IN:(k + 1) * TN_IN] = jax.nn.sigmoid(z).astype(BF16)


def _in_proj(x2, mod3, g1, w_in_p, b_gate, gql, gkvl, gkpe, gqd, gkd, cos, sin, seg, batch, seq):
    t = x2.shape[0]
    tm = TM_IN
    tiles_per_seq = seq // tm
    nk = seq // T_ATT
    row = lambda i, n: (i, 0)
    const = lambda i, n: (0, 0)
    out_shape = [
        jax.ShapeDtypeStruct((t, MLA_Q_RANK), BF16),
        jax.ShapeDtypeStruct((t, MLA_KV_RANK), BF16),
        jax.ShapeDtypeStruct((t, LANES), BF16),
        jax.ShapeDtypeStruct((t, DIFF_HEADS * LANES), BF16),
        jax.ShapeDtypeStruct((t, DIFF_HEADS * LANES), BF16),
        jax.ShapeDtypeStruct((batch, DIFF_HEADS, nk, DIFF_V_DIM, T_ATT), BF16),
        jax.ShapeDtypeStruct((t, 2 * D_MODEL), BF16),
    ]
    out_specs = [
        pl.BlockSpec((tm, MLA_Q_RANK), row),
        pl.BlockSpec((tm, MLA_KV_RANK), row),
        pl.BlockSpec((tm, LANES), row),
        pl.BlockSpec((tm, DIFF_HEADS * LANES), row),
        pl.BlockSpec((tm, DIFF_HEADS * LANES), row),
        pl.BlockSpec((None, DIFF_HEADS, tm // T_ATT, DIFF_V_DIM, T_ATT),
                     lambda i, n: (i // tiles_per_seq, 0, i % tiles_per_seq, 0, 0)),
        pl.BlockSpec((tm, 2 * D_MODEL), row),
    ]
    in_specs = [
        pl.BlockSpec((tm, D_MODEL), row),
        pl.BlockSpec((None, 6, D_MODEL), lambda i, n: (i // tiles_per_seq, 0, 0)),
        pl.BlockSpec((1, D_MODEL), const),
        pl.BlockSpec((D_MODEL, TN_IN), lambda i, n: (0, n)),
        pl.BlockSpec((1, 2 * D_MODEL), const),
        pl.BlockSpec((1, MLA_Q_RANK), const),
        pl.BlockSpec((1, MLA_KV_RANK), const),
        pl.BlockSpec((1, LANES), const),
        pl.BlockSpec((1, LANES), const),
        pl.BlockSpec((1, LANES), const),
        pl.BlockSpec((tm, LANES), row),
        pl.BlockSpec((tm, LANES), row),
        pl.BlockSpec((LANES, LANES), const),
    ]
    return pl.pallas_call(
        _in_proj_kernel,
        grid=(t // tm, IN_PAD_WIDTH // TN_IN),
        in_specs=in_specs,
        out_specs=out_specs,
        out_shape=out_shape,
        scratch_shapes=[pltpu.VMEM((tm, D_MODEL), BF16), pltpu.VMEM((tm, TN_IN), F32)],
        compiler_params=_cparams(("parallel", "arbitrary")),
        name="in_proj",
    )(x2, mod3, g1, w_in_p, b_gate, gql, gkvl, gkpe, gqd, gkd, cos, sin, seg)


def _mla_qkv_kernel(qlat_ref, kvlat_ref, kpe_ref, wq_ref, wk_ref, wvt_ref, gqn_ref, gqp_ref, gkn_ref,
                    cos_ref, sin_ref, q_ref, k_ref, vt_ref):
    tm = qlat_ref.shape[0]
    scale = MLA_QK_DIM ** -0.5 * LOG2E
    q = jnp.dot(qlat_ref[...], wq_ref[...], preferred_element_type=F32)
    kn = jnp.dot(kvlat_ref[...], wk_ref[...], preferred_element_type=F32)
    cos = cos_ref[...]
    sin = sin_ref[...]
    kpe = kpe_ref[...]
    kvl = kvlat_ref[...]
    for hd in range(MLA_HEADS):
        base = hd * MLA_QK_PAD
        qn = q[:, base:base + LANES]
        qp = q[:, base + LANES:base + 2 * LANES]
        qn = _rms(qn, MLA_NOPE_DIM) * gqn_ref[...]
        qp = _rope128(_rms(qp, MLA_ROPE_DIM) * gqp_ref[...], cos, sin)
        q_ref[hd, :, 0:LANES] = (qn * scale).astype(BF16)
        q_ref[hd, :, LANES:2 * LANES] = (qp * scale).astype(BF16)
        kh = kn[:, hd * LANES:(hd + 1) * LANES]
        k_ref[hd, :, 0:LANES] = (_rms(kh, MLA_NOPE_DIM) * gkn_ref[...]).astype(BF16)
        k_ref[hd, :, LANES:2 * LANES] = kpe
        vt = lax.dot_general(wvt_ref[hd], kvl, (((1,), (1,)), ((), ())), preferred_element_type=F32)
        for t in range(tm // T_ATT):
            vt_ref[hd, t] = vt[:, t * T_ATT:(t + 1) * T_ATT].astype(BF16)


def _mla_qkv(qlat, kvlat, kpe, wq, wk, wvt, gqn, gqp, gkn, cos, sin, batch, seq):
    t = qlat.shape[0]
    tm = TM_QKV
    tiles_per_seq = seq // tm
    nk = seq // T_ATT
    row = lambda i: (i, 0)
    const2 = lambda i: (0, 0)
    head_rows = lambda i: (i // tiles_per_seq, 0, i % tiles_per_seq, 0)
    return pl.pallas_call(
        _mla_qkv_kernel,
        grid=(t // tm,),
        in_specs=[
            pl.BlockSpec((tm, MLA_Q_RANK), row),
            pl.BlockSpec((tm, MLA_KV_RANK), row),
            pl.BlockSpec((tm, LANES), row),
            pl.BlockSpec(wq.shape, const2),
            pl.BlockSpec(wk.shape, const2),
            pl.BlockSpec(wvt.shape, lambda i: (0, 0, 0)),
            pl.BlockSpec((1, LANES), const2),
            pl.BlockSpec((1, LANES), const2),
            pl.BlockSpec((1, LANES), const2),
            pl.BlockSpec((tm, LANES), row),
            pl.BlockSpec((tm, LANES), row),
        ],
        out_specs=[
            pl.BlockSpec((None, MLA_HEADS, tm, MLA_QK_PAD), head_rows),
            pl.BlockSpec((None, MLA_HEADS, tm, MLA_QK_PAD), head_rows),
            pl.BlockSpec((None, MLA_HEADS, tm // T_ATT, MLA_V_DIM, T_ATT),
                         lambda i: (i // tiles_per_seq, 0, i % tiles_per_seq, 0, 0)),
        ],
        out_shape=[
            jax.ShapeDtypeStruct((batch, MLA_HEADS, seq, MLA_QK_PAD), BF16),
            jax.ShapeDtypeStruct((batch, MLA_HEADS, seq, MLA_QK_PAD), BF16),
            jax.ShapeDtypeStruct((batch, MLA_HEADS, nk, MLA_V_DIM, T_ATT), BF16),
        ],
        compiler_params=_cparams(("parallel",)),
        name="mla_qkv",
    )(qlat, kvlat, kpe, wq, wk, wvt, gqn, gqp, gkn, cos, sin)


_NT = (((1,), (1,)), ((), ()))


def _score_tile(k, q, s_ref, mt_ref):
    st = lax.dot_general(k, q, _NT, preferred_element_type=F32)
    s_ref[...] = st
    mt_ref[...] = jnp.max(st, axis=0, keepdims=True)


def _softmax_tile(s_ref, mt_ref, vt, m_ref, l_ref, acc_ref, masked):
    st = s_ref[...]
    if masked:
        key = lax.broadcasted_iota(jnp.int32, st.shape, 0)
        qry = lax.broadcasted_iota(jnp.int32, st.shape, 1)
        st = jnp.where(key <= qry, st, NEG_INF)
        mt = jnp.max(st, axis=0, keepdims=True)
    else:
        mt = mt_ref[...]
    m_prev = m_ref[...]
    m_new = jnp.maximum(m_prev, mt)
    alpha = jnp.exp2(m_prev - m_new)
    p = jnp.exp2(st - m_new)
    l_ref[...] = alpha * l_ref[...] + jnp.sum(p, axis=0, keepdims=True)
    acc_ref[...] = alpha * acc_ref[...] + jnp.dot(vt, p.astype(BF16), preferred_element_type=F32)
    m_ref[...] = m_new


def _causal_tile_sweep(i, score, softmax):
    score(0, 0)

    def pair(jj, carry):
        j = 2 * jj
        score(j + 1, 1)
        softmax(j, 0, False)
        score(j + 2, 0)
        softmax(j + 1, 1, False)
        return carry

    lax.fori_loop(0, i // 2, pair, 0)

    @pl.when(i % 2 == 0)
    def _():
        softmax(i, 0, True)

    @pl.when(i % 2 == 1)
    def _():
        score(i, 1)
        softmax(i - 1, 0, False)
        softmax(i, 1, True)


def _key_tile(k_ref, j):
    return k_ref[pl.ds(pl.multiple_of(j * T_ATT, T_ATT), T_ATT), :]


def _mla_attn_kernel(q_ref, k_ref, vt_ref, o_ref, m_ref, l_ref, acc_ref, sa_ref, sb_ref, mta_ref, mtb_ref):
    i = pl.program_id(2)
    m_ref[...] = jnp.full(m_ref.shape, NEG_INF, F32)
    l_ref[...] = jnp.zeros(l_ref.shape, F32)
    acc_ref[...] = jnp.zeros(acc_ref.shape, F32)
    bufs = ((sa_ref, mta_ref), (sb_ref, mtb_ref))

    def score(j, b):
        _score_tile(_key_tile(k_ref, j), q_ref[...], *bufs[b])

    def softmax(j, b, masked):
        _softmax_tile(*bufs[b], vt_ref[j], m_ref, l_ref, acc_ref, masked)

    _causal_tile_sweep(i, score, softmax)
    o = acc_ref[...] * (1.0 / l_ref[...])
    o_ref[...] = o.T.astype(BF16)


def _mla_attention(q, k, vt, batch, seq):
    nq = seq // T_ATT
    return pl.pallas_call(
        _mla_attn_kernel,
        grid=(batch, MLA_HEADS, nq),
        in_specs=[
            pl.BlockSpec((None, None, T_ATT, MLA_QK_PAD), lambda b, h, i: (b, h, i, 0)),
            pl.BlockSpec((None, None, seq, MLA_QK_PAD), lambda b, h, i: (b, h, 0, 0)),
            pl.BlockSpec((None, None, nq, MLA_V_DIM, T_ATT), lambda b, h, i: (b, h, 0, 0, 0)),
        ],
        out_specs=pl.BlockSpec((None, T_ATT, MLA_V_DIM), lambda b, h, i: (b, i, h)),
        out_shape=jax.ShapeDtypeStruct((batch, seq, MLA_HEADS * MLA_V_DIM), BF16),
        scratch_shapes=[pltpu.VMEM((1, T_ATT), F32), pltpu.VMEM((1, T_ATT), F32),
                        pltpu.VMEM((MLA_V_DIM, T_ATT), F32),
                        pltpu.VMEM((T_ATT, T_ATT), F32), pltpu.VMEM((T_ATT, T_ATT), F32),
                        pltpu.VMEM((1, T_ATT), F32), pltpu.VMEM((1, T_ATT), F32)],
        compiler_params=_cparams(("parallel", "parallel", "arbitrary")),
        name="mla_attn",
    )(q, k, vt)


def _diff_attn_kernel(q_ref, k_ref, vt_ref, lam_ref, gsub_ref, o_ref,
                      m1_ref, l1_ref, acc1_ref, m2_ref, l2_ref, acc2_ref, q1_ref, q2_ref,
                      s1a_ref, s1b_ref, s2a_ref, s2b_ref, mt1a_ref, mt1b_ref, mt2a_ref, mt2b_ref):
    i = pl.program_id(2)
    q = q_ref[...]
    lane = lax.broadcasted_iota(jnp.int32, q.shape, 1)
    zero = jnp.zeros_like(q)
    q1_ref[...] = jnp.where(lane < DIFF_HEAD_DIM, q, zero)
    q2_ref[...] = jnp.where(lane >= DIFF_HEAD_DIM, q, zero)
    for m_ref, l_ref, acc_ref in ((m1_ref, l1_ref, acc1_ref), (m2_ref, l2_ref, acc2_ref)):
        m_ref[...] = jnp.full(m_ref.shape, NEG_INF, F32)
        l_ref[...] = jnp.zeros(l_ref.shape, F32)
        acc_ref[...] = jnp.zeros(acc_ref.shape, F32)
    maps = ((q1_ref, m1_ref, l1_ref, acc1_ref, ((s1a_ref, mt1a_ref), (s1b_ref, mt1b_ref))),
            (q2_ref, m2_ref, l2_ref, acc2_ref, ((s2a_ref, mt2a_ref), (s2b_ref, mt2b_ref))))

    def score(j, b):
        k = _key_tile(k_ref, j)
        for qm_ref, _, _, _, bufs in maps:
            _score_tile(k, qm_ref[...], *bufs[b])

    def softmax(j, b, masked):
        vt = vt_ref[j]
        for _, m_ref, l_ref, acc_ref, bufs in maps:
            _softmax_tile(*bufs[b], vt, m_ref, l_ref, acc_ref, masked)

    _causal_tile_sweep(i, score, softmax)

    lv = lam_ref[...]
    lam = (jnp.exp(jnp.sum(lv[0:1, :] * lv[1:2, :], axis=-1, keepdims=True))
           - jnp.exp(jnp.sum(lv[2:3, :] * lv[3:4, :], axis=-1, keepdims=True)) + LAMBDA_INIT)
    o = acc1_ref[...] * (1.0 / l1_ref[...]) - lam * (acc2_ref[...] * (1.0 / l2_ref[...]))
    ot = o.T
    ot = _rms(ot, DIFF_V_DIM) * gsub_ref[...] * (1.0 - LAMBDA_INIT)
    o_ref[...] = ot.astype(BF16)


def _diff_attention(dq, dk, dvt, lam4, gsub, batch, seq):
    nq = seq // T_ATT
    return pl.pallas_call(
        _diff_attn_kernel,
        grid=(batch, DIFF_HEADS, nq),
        in_specs=[
            pl.BlockSpec((None, T_ATT, LANES), lambda b, h, i: (b, i, h)),
            pl.BlockSpec((None, seq, LANES), lambda b, h, i: (b, 0, h)),
            pl.BlockSpec((None, None, nq, DIFF_V_DIM, T_ATT), lambda b, h, i: (b, h, 0, 0, 0)),
            pl.BlockSpec(lam4.shape, lambda b, h, i: (0, 0)),
            pl.BlockSpec((1, LANES), lambda b, h, i: (0, 0)),
        ],
        out_specs=pl.BlockSpec((None, T_ATT, DIFF_V_DIM), lambda b, h, i: (b, i, h)),
        out_shape=jax.ShapeDtypeStruct((batch, seq, DIFF_HEADS * DIFF_V_DIM), BF16),
        scratch_shapes=[pltpu.VMEM((1, T_ATT), F32), pltpu.VMEM((1, T_ATT), F32),
                        pltpu.VMEM((DIFF_V_DIM, T_ATT), F32),
                        pltpu.VMEM((1, T_ATT), F32), pltpu.VMEM((1, T_ATT), F32),
                        pltpu.VMEM((DIFF_V_DIM, T_ATT), F32),
                        pltpu.VMEM((T_ATT, LANES), BF16), pltpu.VMEM((T_ATT, LANES), BF16)]
                       + [pltpu.VMEM((T_ATT, T_ATT), F32)] * 4 + [pltpu.VMEM((1, T_ATT), F32)] * 4,
        compiler_params=_cparams(("parallel", "parallel", "arbitrary")),
        name="diff_attn",
    )(dq, dk, dvt, lam4, gsub)


def _merge_kernel(x_ref, mod_ref, oa_ref, ob_ref, gate_ref, woa_ref, wob_ref, wout_ref, o_ref):
    a = jnp.dot(oa_ref[...], woa_ref[...], preferred_element_type=F32)
    b = jnp.dot(ob_ref[...], wob_ref[...], preferred_element_type=F32)
    ga = gate_ref[:, 0:D_MODEL].astype(F32)
    gb = gate_ref[:, D_MODEL:2 * D_MODEL].astype(F32)
    mixed = (ga * a + gb * b).astype(BF16)
    y = jnp.dot(mixed, wout_ref[...], preferred_element_type=F32)
    o_ref[...] = x_ref[...] + mod_ref[2:3, :] * y


def _merge(x2, mod3, oa, ob, gates, woa, wob, wout, seq):
    t = x2.shape[0]
    tm = TM_MERGE
    tiles_per_seq = seq // tm
    row = lambda i: (i, 0)
    const = lambda i: (0, 0)
    resident = dict(pipeline_mode=pl.Buffered(1))
    return pl.pallas_call(
        _merge_kernel,
        grid=(t // tm,),
        in_specs=[
            pl.BlockSpec((tm, D_MODEL), row),
            pl.BlockSpec((None, 6, D_MODEL), lambda i: (i // tiles_per_seq, 0, 0)),
            pl.BlockSpec((tm, oa.shape[1]), row),
            pl.BlockSpec((tm, ob.shape[1]), row),
            pl.BlockSpec((tm, 2 * D_MODEL), row),
            pl.BlockSpec(woa.shape, const, **resident),
            pl.BlockSpec(wob.shape, const, **resident),
            pl.BlockSpec(wout.shape, const, **resident),
        ],
        out_specs=pl.BlockSpec((tm, D_MODEL), row),
        out_shape=jax.ShapeDtypeStruct((t, D_MODEL), F32),
        compiler_params=_cparams(("parallel",)),
        name="merge_out",
    )(x2, mod3, oa, ob, gates, woa, wob, wout)


def _ffn_kernel(x_ref, halo_ref, mod_ref, g2_ref, wv_ref, wg_ref, cwv_ref, cwg_ref, cbv_ref, cbg_ref,
                wd_ref, o_ref, h_ref, *, tiles_per_seq):
    i = pl.program_id(0)
    j = pl.program_id(1)
    tm = x_ref.shape[0]

    @pl.when(j == 0)
    def _():
        scale = 1.0 + mod_ref[4:5, :]
        shift = mod_ref[3:4, :]
        h_ref[HALO:, :] = (_rms(x_ref[...], D_MODEL) * g2_ref[...] * scale + shift).astype(BF16)
        hh = (_rms(halo_ref[...], D_MODEL) * g2_ref[...] * scale + shift).astype(BF16)
        first = (i % tiles_per_seq) == 0
        h_ref[0:HALO, :] = jnp.where(first, jnp.zeros_like(hh), hh)
        o_ref[...] = jnp.zeros(o_ref.shape, F32)

    h = h_ref[...]

    def conv(w_ref, cw_ref, cb_ref):
        u = jnp.dot(h, w_ref[...], preferred_element_type=F32)
        u1 = pltpu.roll(u, 1, axis=0)
        u2 = pltpu.roll(u, 2, axis=0)
        return (cb_ref[...] + cw_ref[2:3, :] * u[HALO:, :] + cw_ref[1:2, :] * u1[HALO:, :]
                + cw_ref[0:1, :] * u2[HALO:, :])

    val = conv(wv_ref, cwv_ref, cbv_ref)
    gte = conv(wg_ref, cwg_ref, cbg_ref)
    act = (gte * jax.nn.sigmoid(gte) * val).astype(BF16)
    o_ref[...] += jnp.dot(act, wd_ref[...], preferred_element_type=F32)

    @pl.when(j == pl.num_programs(1) - 1)
    def _():
        o_ref[...] = x_ref[...] + mod_ref[5:6, :] * o_ref[...]


def _ffn(x1, mod3, g2, wv, wg, cwv, cwg, cbv, cbg, wd, seq):
    t = x1.shape[0]
    tm, tf = TM_FFN, TF_FFN
    tiles_per_seq = seq // tm
    halo_blocks = tm // HALO
    row = lambda i, j: (i, 0)
    return pl.pallas_call(
        functools.partial(_ffn_kernel, tiles_per_seq=tiles_per_seq),
        grid=(t // tm, D_FF // tf),
        in_specs=[
            pl.BlockSpec((tm, D_MODEL), row),
            pl.BlockSpec((HALO, D_MODEL), lambda i, j: (jnp.maximum(i * halo_blocks - 1, 0), 0)),
            pl.BlockSpec((None, 6, D_MODEL), lambda i, j: (i // tiles_per_seq, 0, 0)),
            pl.BlockSpec((1, D_MODEL), lambda i, j: (0, 0)),
            pl.BlockSpec((D_MODEL, tf), lambda i, j: (0, j)),
            pl.BlockSpec((D_MODEL, tf), lambda i, j: (0, j)),
            pl.BlockSpec((CONV_WIDTH, tf), lambda i, j: (0, j)),
            pl.BlockSpec((CONV_WIDTH, tf), lambda i, j: (0, j)),
            pl.BlockSpec((1, tf), lambda i, j: (0, j)),
            pl.BlockSpec((1, tf), lambda i, j: (0, j)),
            pl.BlockSpec((tf, D_MODEL), lambda i, j: (j, 0)),
        ],
        out_specs=pl.BlockSpec((tm, D_MODEL), row),
        out_shape=jax.ShapeDtypeStruct((t, D_MODEL), F32),
        scratch_shapes=[pltpu.VMEM((HALO + tm, D_MODEL), BF16)],
        compiler_params=_cparams(("parallel", "arbitrary")),
        name="conv_ffn",
    )(x1, x1, mod3, g2, wv, wg, cwv, cwg, cbv, cbg, wd)


def _layer(x, c, positions, w_ada, b_ada, g_norm1, w_in, b_gate, g_q_lat, w_q_up, g_kv_lat, w_kv_up,
           g_q_mla, g_k_mla, w_o_mla, g_q_diff, g_k_diff, lam_q1, lam_k1, lam_q2, lam_k2, g_sub_diff,
           w_o_diff, w_out, g_norm2, w_up, conv_w, conv_b, w_down):
    batch, seq, d = x.shape
    t = batch * seq
    assert d == D_MODEL and seq % TM_IN == 0 and seq % T_ATT == 0 and TM_IN % T_ATT == 0
    assert seq % TM_FFN == 0 and seq % TM_MERGE == 0 and seq % TM_QKV == 0 and TM_QKV % T_ATT == 0
    assert t % 2048 == 0 and D_FF % TF_FFN == 0
    x2 = x.reshape(t, d)
    row = lambda v: v.reshape(1, -1).astype(F32)

    half = DIFF_HEAD_DIM // 2
    inv_freq = ROPE_THETA ** (-jnp.arange(0, DIFF_HEAD_DIM, 2, dtype=F32) / DIFF_HEAD_DIM)
    invf128 = jnp.tile(inv_freq, LANES // half).reshape(1, LANES)
    sgn128 = jnp.tile(jnp.concatenate([-jnp.ones((half,), F32), jnp.ones((half,), F32)]),
                      LANES // DIFF_HEAD_DIM).reshape(1, LANES)
    pos128 = jnp.broadcast_to(positions.reshape(t, 1), (t, LANES))
    cos, sin = _rope_table(pos128, invf128, sgn128)

    c_pad = jnp.zeros((8, d), F32).at[:batch].set(c)
    mod = _ada_mod(c_pad, w_ada.astype(BF16), row(b_ada))
    mod3 = mod[:batch].reshape(batch, 6, d)

    small = MLA_Q_RANK + MLA_KV_RANK + MLA_ROPE_DIM
    w_in_p = jnp.concatenate([w_in[:, :small], jnp.zeros((d, TN_IN - small), w_in.dtype),
                              w_in[:, small:]], axis=1).astype(BF16)
    pad64 = jnp.zeros((LANES - MLA_ROPE_DIM,), F32)
    gkpe = jnp.concatenate([g_k_mla[MLA_NOPE_DIM:], pad64]).reshape(1, LANES)
    seg = (jnp.arange(LANES)[:, None] // DIFF_HEAD_DIM == jnp.arange(LANES)[None, :] // DIFF_HEAD_DIM)
    qlat, kvlat, kpe, dq, dk, dvt, gates = _in_proj(
        x2, mod3, row(g_norm1), w_in_p, row(b_gate), row(g_q_lat), row(g_kv_lat), gkpe,
        row(jnp.tile(g_q_diff, 2)), row(jnp.tile(g_k_diff, 2)), cos, sin, seg.astype(BF16), batch, seq)

    wq = w_q_up.reshape(MLA_Q_RANK, MLA_HEADS, MLA_QK_DIM)
    wq = jnp.pad(wq, ((0, 0), (0, 0), (0, MLA_QK_PAD - MLA_QK_DIM)))
    wq = wq.reshape(MLA_Q_RANK, MLA_HEADS * MLA_QK_PAD).astype(BF16)
    wkv = w_kv_up.reshape(MLA_KV_RANK, MLA_HEADS, MLA_NOPE_DIM + MLA_V_DIM)
    wk = wkv[:, :, :MLA_NOPE_DIM].reshape(MLA_KV_RANK, MLA_HEADS * MLA_NOPE_DIM).astype(BF16)
    wvt = jnp.transpose(wkv[:, :, MLA_NOPE_DIM:], (1, 2, 0)).astype(BF16)
    gqp = jnp.concatenate([g_q_mla[MLA_NOPE_DIM:], pad64]).reshape(1, LANES)
    q_mla, k_mla, vt_mla = _mla_qkv(qlat, kvlat, kpe, wq, wk, wvt, row(g_q_mla[:MLA_NOPE_DIM]), gqp,
                                    row(g_k_mla[:MLA_NOPE_DIM]), cos, sin, batch, seq)

    o_mla = _mla_attention(q_mla, k_mla, vt_mla, batch, seq)
    lam4 = jnp.stack([lam_q1, lam_k1, lam_q2, lam_k2]).astype(F32)
    o_diff = _diff_attention(dq.reshape(batch, seq, -1), dk.reshape(batch, seq, -1), dvt, lam4,
                             row(g_sub_diff), batch, seq)

    x1 = _merge(x2, mod3, o_mla.reshape(t, -1), o_diff.reshape(t, -1), gates,
                w_o_mla.astype(BF16), w_o_diff.astype(BF16), w_out.astype(BF16), seq)

    out = _ffn(x1, mod3, row(g_norm2), w_up[:, :D_FF].astype(BF16), w_up[:, D_FF:].astype(BF16),
               conv_w[:, :D_FF], conv_w[:, D_FF:], row(conv_b[:D_FF]), row(conv_b[D_FF:]),
               w_down.astype(BF16), seq)
    return out.reshape(batch, seq, d)


def kernel(x, c, positions, w_ada, b_ada, g_norm1, w_in, b_gate, g_q_lat, w_q_up, g_kv_lat, w_kv_up, g_q_mla, g_k_mla, w_o_mla, g_q_diff, g_k_diff, lam_q1, lam_k1, lam_q2, lam_k2, g_sub_diff, w_o_diff, w_out, g_norm2, w_up, conv_w, conv_b, w_down):
    depth = w_ada.shape[0]
    assert depth == 1, "lambda_init is derived for a single layer"
    return _layer(x, c, positions, w_ada[0], b_ada[0], g_norm1[0], w_in[0], b_gate[0], g_q_lat[0],
                  w_q_up[0], g_kv_lat[0], w_kv_up[0], g_q_mla[0], g_k_mla[0], w_o_mla[0], g_q_diff[0],
                  g_k_diff[0], lam_q1[0], lam_k1[0], lam_q2[0], lam_k2[0], g_sub_diff[0], w_o_diff[0],
                  w_out[0], g_norm2[0], w_up[0], conv_w[0], conv_b[0], w_down[0])
```
